```python
import jax, jax.numpy as jnp
from jax import lax
import numpy as np

D_MODEL = 1024
BATCH = 4
SEQ = 8192
DEPTH = 2

CHUNK = 64
RET_HEADS = 4
RET_QK_DIM = D_MODEL // RET_HEADS
RET_V_WIDTH = 2 * D_MODEL
RET_V_DIM = RET_V_WIDTH // RET_HEADS
CONV_WIDTH = D_MODEL
CONV_KERNEL = 31
N_BRANCHES = 2
IN_COLS = 2 * D_MODEL + 2 * RET_V_WIDTH + 2 * CONV_WIDTH + N_BRANCHES * D_MODEL
SPLITS = (
    D_MODEL,
    2 * D_MODEL,
    2 * D_MODEL + RET_V_WIDTH,
    2 * D_MODEL + 2 * RET_V_WIDTH,
    2 * D_MODEL + 2 * RET_V_WIDTH + CONV_WIDTH,
    2 * D_MODEL + 2 * RET_V_WIDTH + 2 * CONV_WIDTH,
)
D_FF = 2816
N_EXPERTS = 8
TOP_K = 2
EXPERT_D_FF = 1408
N_DENSE = (DEPTH + 1) // 2
N_MOE = DEPTH // 2
ROPE_BASE = 10000.0
EPS = 1e-6

kernel_name = "hybrid_retention_conformer_moe_trunk"


def rmsnorm(x, g):
    xf = x.astype(jnp.float32)
    y = xf * lax.rsqrt(jnp.mean(xf * xf, axis=-1, keepdims=True) + EPS)
    return (y * g.astype(jnp.float32)).astype(x.dtype)


def layernorm(x, g, b):
    xf = x.astype(jnp.float32)
    mu = jnp.mean(xf, axis=-1, keepdims=True)
    xc = xf - mu
    y = xc * lax.rsqrt(jnp.mean(xc * xc, axis=-1, keepdims=True) + EPS)
    return (y * g.astype(jnp.float32) + b.astype(jnp.float32)).astype(x.dtype)


def rotary(t, pos):
    half = t.shape[-1] // 2
    inv_freq = ROPE_BASE ** (-jnp.arange(half, dtype=jnp.float32) / half)
    ang = pos[:, None] * inv_freq[None, :]
    cos = jnp.cos(ang)[None, :, None, :]
    sin = jnp.sin(ang)[None, :, None, :]
    tf = t.astype(jnp.float32)
    t1, t2 = tf[..., :half], tf[..., half:]
    out = jnp.concatenate([t1 * cos - t2 * sin, t1 * sin + t2 * cos], axis=-1)
    return out.astype(t.dtype)


def retention(q, k, v):
    b, s, h, dk = q.shape
    dv = v.shape[-1]
    n = s // CHUNK
    log_g = jnp.log1p(-jnp.exp2(-5.0 - jnp.arange(h, dtype=jnp.float32)))

    def to_chunks(t):
        return t.astype(jnp.float32).reshape(b, n, CHUNK, h, t.shape[-1]).transpose(0, 3, 1, 2, 4)

    qc, kc, vc = to_chunks(q), to_chunks(k), to_chunks(v)
    idx = jnp.arange(CHUNK, dtype=jnp.float32)
    intra_decay = jnp.exp(log_g[:, None, None] * jnp.abs(idx[:, None] - idx[None, :]))
    scores = jnp.einsum('bhnid,bhnjd->bhnij', qc, kc) * intra_decay[None, :, None]
    intra = jnp.einsum('bhnij,bhnje->bhnie', scores, vc)

    q_decay = jnp.exp(log_g[:, None] * (idx[None, :] + 1.0))
    k_decay = jnp.exp(log_g[:, None] * (CHUNK - 1.0 - idx[None, :]))
    chunk_decay = jnp.exp(log_g * CHUNK)

    def step(state, inp):
        qn, kn, vn = inp
        cross = jnp.einsum('bhid,bhde->bhie', qn, state) * q_decay[None, :, :, None]
        kv = jnp.einsum('bhjd,bhje->bhde', kn * k_decay[None, :, :, None], vn)
        state = state * chunk_decay[None, :, None, None] + kv
        return state, cross

    xs = (qc.transpose(2, 0, 1, 3, 4), kc.transpose(2, 0, 1, 3, 4), vc.transpose(2, 0, 1, 3, 4))
    state0 = jnp.zeros((b, h, dk, dv), jnp.float32)
    _, cross = lax.scan(step, state0, xs)
    out = intra + cross.transpose(1, 2, 0, 3, 4)
    return out.transpose(0, 2, 3, 1, 4).reshape(b, s, h, dv)


def mixer(x, norm_g, w_in, b_gate, conv_w, conv_b, conv_ln_g, conv_ln_b,
          w_ret_o, w_conv_o, b_conv_o, w_out):
    b, s, _ = x.shape
    hn = rmsnorm(x, norm_g)
    proj = hn @ w_in
    q, k, v, g_ret_out, glu_val, glu_gate, gate_logits = jnp.split(proj, SPLITS, axis=-1)

    pos = jnp.arange(s, dtype=jnp.float32)
    q = rotary(q.reshape(b, s, RET_HEADS, RET_QK_DIM), pos)
    k = rotary(k.reshape(b, s, RET_HEADS, RET_QK_DIM), pos) * (RET_QK_DIM ** -0.5)
    v = v.reshape(b, s, RET_HEADS, RET_V_DIM)
    r = retention(q, k, v)
    r = r * lax.rsqrt(jnp.mean(r * r, axis=-1, keepdims=True) + EPS)
    r = (jax.nn.silu(g_ret_out.astype(jnp.float32)) * r.reshape(b, s, RET_V_WIDTH)).astype(x.dtype)
    branch_ret = r @ w_ret_o

    u = glu_val * jax.nn.sigmoid(glu_gate)
    u = lax.conv_general_dilated(
        u, conv_w[:, None, :], window_strides=(1,), padding=[(CONV_KERNEL - 1, 0)],
        dimension_numbers=('NWC', 'WIO', 'NWC'), feature_group_count=CONV_WIDTH) + conv_b
    u = jax.nn.silu(layernorm(u, conv_ln_g, conv_ln_b))
    branch_conv = u @ w_conv_o + b_conv_o

    gates = jax.nn.sigmoid(gate_logits + b_gate)
    g_a, g_b = jnp.split(gates, N_BRANCHES, axis=-1)
    merged = g_a * branch_ret + g_b * branch_conv
    return merged @ w_out


def swiglu(h, w1, w3, w2):
    return (jax.nn.silu(h @ w1) * (h @ w3)) @ w2


def moe(h, router, w1, w3, w2):
    logits = (h @ router).astype(jnp.float32)
    top_val, top_idx = lax.top_k(logits, TOP_K)
    top_w = jax.nn.softmax(top_val, axis=-1)
    gate = jnp.sum(jax.nn.one_hot(top_idx, N_EXPERTS, dtype=jnp.float32) * top_w[..., None], axis=-2)
    gate = gate.astype(h.dtype)
    out = jnp.zeros_like(h)
    for e in range(N_EXPERTS):
        out = out + gate[..., e:e + 1] * swiglu(h, w1[e], w3[e], w2[e])
    return out


def setup_inputs(seed: int = 0) -> dict:
    key = jax.random.key(seed)
    ks = jax.random.split(key, 24)
    f32 = jnp.float32
    D = D_MODEL

    def nrm(k, shape, fan_in):
        return jax.random.normal(k, shape, f32) * (fan_in ** -0.5)

    def gain(k, shape):
        return 1.0 + 0.01 * jax.random.normal(k, shape, f32)

    def small(k, shape):
        return 0.01 * jax.random.normal(k, shape, f32)

    return {
        "x": jax.random.normal(ks[0], (BATCH, SEQ, D), f32),
        "mix_norm": gain(ks[1], (DEPTH, D)),
        "w_in": nrm(ks[2], (DEPTH, D, IN_COLS), D),
        "b_gate": small(ks[3], (DEPTH, N_BRANCHES * D)),
        "conv_w": nrm(ks[4], (DEPTH, CONV_KERNEL, CONV_WIDTH), CONV_KERNEL),
        "conv_b": small(ks[5], (DEPTH, CONV_WIDTH)),
        "conv_ln_g": gain(ks[6], (DEPTH, CONV_WIDTH)),
        "conv_ln_b": small(ks[7], (DEPTH, CONV_WIDTH)),
        "w_ret_o": nrm(ks[8], (DEPTH, RET_V_WIDTH, D), RET_V_WIDTH),
        "w_conv_o": nrm(ks[9], (DEPTH, CONV_WIDTH, D), CONV_WIDTH),
        "b_conv_o": small(ks[10], (DEPTH, D)),
        "w_out": nrm(ks[11], (DEPTH, D, D), D),
        "ffn_norm": gain(ks[12], (DEPTH, D)),
        "dense_w1": nrm(ks[13], (N_DENSE, D, D_FF), D),
        "dense_w3": nrm(ks[14], (N_DENSE, D, D_FF), D),
        "dense_w2": nrm(ks[15], (N_DENSE, D_FF, D), D_FF),
        "moe_router": nrm(ks[16], (N_MOE, D, N_EXPERTS), D),
        "moe_w1": nrm(ks[17], (N_MOE, N_EXPERTS, D, EXPERT_D_FF), D),
        "moe_w3": nrm(ks[18], (N_MOE, N_EXPERTS, D, EXPERT_D_FF), D),
        "moe_w2": nrm(ks[19], (N_MOE, N_EXPERTS, EXPERT_D_FF, D), EXPERT_D_FF),
        "final_norm": gain(ks[20], (D,)),
    }


def reference(x, mix_norm, w_in, b_gate, conv_w, conv_b, conv_ln_g, conv_ln_b,
              w_ret_o, w_conv_o, b_conv_o, w_out, ffn_norm,
              dense_w1, dense_w3, dense_w2, moe_router, moe_w1, moe_w3, moe_w2,
              final_norm):
    for i in range(DEPTH):
        x = x + mixer(x, mix_norm[i], w_in[i], b_gate[i], conv_w[i], conv_b[i],
                      conv_ln_g[i], conv_ln_b[i], w_ret_o[i], w_conv_o[i],
                      b_conv_o[i], w_out[i])
        hn = rmsnorm(x, ffn_norm[i])
        j = i // 2
        if i % 2 == 0:
            x = x + swiglu(hn, dense_w1[j], dense_w3[j], dense_w2[j])
        else:
            x = x + moe(hn, moe_router[j], moe_w1[j], moe_w3[j], moe_w2[j])
    return rmsnorm(x, final_norm)
```

```python
import functools

import numpy as np
import jax
import jax.numpy as jnp
from jax import lax
from jax.experimental import pallas as pl
from jax.experimental.pallas import tpu as pltpu

F32 = jnp.float32
BF16 = jnp.bfloat16

D_MODEL = 1024
CHUNK = 64
RET_HEADS = 4
RET_QK_DIM = 256
RET_V_DIM = 512
RET_V_WIDTH = 2048
CONV_WIDTH = 1024
CONV_KERNEL = 31
IN_COLS = 10240
N_EXPERTS = 8
ROPE_BASE = 10000.0
EPS = 1e-6

LANES = 128
VMEM_LIMIT = 56 * 1024 * 1024

COL_Q, COL_K, COL_V, COL_G = 0, 1024, 2048, 4096
COL_GLU_VAL, COL_GLU_GATE, COL_MERGE = 6144, 7168, 8192


def _params(*sem):
    return pltpu.CompilerParams(dimension_semantics=sem, vmem_limit_bytes=VMEM_LIMIT)


def _rms(x, g):
    return x * lax.rsqrt(jnp.mean(x * x, axis=-1, keepdims=True) + EPS) * g


def _sigmoid(x):
    return 1.0 / (1.0 + jnp.exp(-x))


def _silu(x):
    return x * _sigmoid(x)


def _proj_kernel(x_ref, g_ref, w_ref, o_ref, h_ref):
    @pl.when(pl.program_id(1) == 0)
    def _():
        h_ref[...] = _rms(x_ref[...], g_ref[...]).astype(BF16)

    o_ref[...] = jnp.dot(h_ref[...], w_ref[...],
                         preferred_element_type=F32).astype(o_ref.dtype)


def _proj(x, g, w, bm=1024, bn=1024):
    t, d = x.shape
    n = w.shape[1]
    return pl.pallas_call(
        _proj_kernel,
        out_shape=jax.ShapeDtypeStruct((t, n), BF16),
        grid=(t // bm, n // bn),
        in_specs=[pl.BlockSpec((bm, d), lambda i, j: (i, 0)),
                  pl.BlockSpec((1, d), lambda i, j: (0, 0)),
                  pl.BlockSpec((d, bn), lambda i, j: (0, j))],
        out_specs=pl.BlockSpec((bm, bn), lambda i, j: (i, j)),
        scratch_shapes=[pltpu.VMEM((bm, d), BF16)],
        compiler_params=_params("parallel", "arbitrary"),
        name="norm_in_proj",
    )(x, g, w)


RET_BLOCK = 256
RET_STEP = 512


def _retention_tables():
    h = np.arange(RET_HEADS, dtype=np.float64)
    log_g = np.log1p(-np.exp2(-5.0 - h))
    idx = np.arange(RET_BLOCK, dtype=np.float64)
    diff = idx[:, None] - idx[None, :]
    ci, cj = (idx // CHUNK)[:, None], (idx // CHUNK)[None, :]
    expo = np.where(ci == cj, np.abs(diff), diff)
    dmat = np.where(ci >= cj, np.exp(log_g[:, None, None] * expo[None]), 0.0)
    dq = np.exp(log_g[:, None] * (idx[None, :] + 1.0))[..., None]
    dk = np.exp(log_g[:, None] * (RET_BLOCK - 1.0 - idx[None, :]))[..., None]
    block_decay = [float(v) for v in np.exp(log_g * RET_BLOCK)]
    return (jnp.asarray(dmat, F32), jnp.asarray(dq, F32), jnp.asarray(dk, F32),
            block_decay)


def _retention_kernel(block_decay, q_ref, k_ref, v_ref, g_ref, cos_ref, sin_ref,
                      dmat_ref, dq_ref, dk_ref, o_ref, state_ref):
    @pl.when(pl.program_id(1) == 0)
    def _():
        state_ref[...] = jnp.zeros_like(state_ref)

    half = RET_QK_DIM // 2
    for c in range(RET_STEP // RET_BLOCK):
        rows = slice(c * RET_BLOCK, (c + 1) * RET_BLOCK)
        cos = cos_ref[rows, :]
        sin = sin_ref[rows, :]

        def rot(ref, col):
            t1 = ref[rows, col:col + half].astype(F32)
            t2 = ref[rows, col + half:col + 2 * half].astype(F32)
            return jnp.concatenate([t1 * cos - t2 * sin, t1 * sin + t2 * cos], axis=-1)

        for h in range(RET_HEADS):
            q = rot(q_ref, h * RET_QK_DIM).astype(BF16)
            kf = rot(k_ref, h * RET_QK_DIM) * (RET_QK_DIM ** -0.5)
            v = v_ref[rows, h * RET_V_DIM:(h + 1) * RET_V_DIM]
            s = lax.dot_general(q, kf.astype(BF16), (((1,), (1,)), ((), ())),
                                preferred_element_type=F32)
            s = (s * dmat_ref[h]).astype(BF16)
            out = jnp.dot(s, v, preferred_element_type=F32)
            state = state_ref[h]
            out = out + jnp.dot(q, state.astype(BF16),
                                preferred_element_type=F32) * dq_ref[h]
            kd = (kf * dk_ref[h]).astype(BF16)
            kv = lax.dot_general(kd, v, (((0,), (0,)), ((), ())),
                                 preferred_element_type=F32)
            state_ref[h] = state * block_decay[h] + kv
            out = out * lax.rsqrt(jnp.mean(out * out, axis=-1, keepdims=True) + EPS)
            gate = g_ref[rows, h * RET_V_DIM:(h + 1) * RET_V_DIM].astype(F32)
            o_ref[rows, h * RET_V_DIM:(h + 1) * RET_V_DIM] = (
                _silu(gate) * out).astype(o_ref.dtype)


def _retention(proj, cos, sin, batch, seq):
    t = proj.shape[0]
    ns = seq // RET_STEP
    dmat, dq, dk, block_decay = _retention_tables()
    row = lambda b, s: b * ns + s
    const3 = lambda b, s: (0, 0, 0)
    return pl.pallas_call(
        functools.partial(_retention_kernel, block_decay),
        out_shape=jax.ShapeDtypeStruct((t, RET_V_WIDTH), BF16),
        grid=(batch, ns),
        in_specs=[
            pl.BlockSpec((RET_STEP, 1024), lambda b, s: (row(b, s), COL_Q // 1024)),
            pl.BlockSpec((RET_STEP, 1024), lambda b, s: (row(b, s), COL_K // 1024)),
            pl.BlockSpec((RET_STEP, 2048), lambda b, s: (row(b, s), COL_V // 2048)),
            pl.BlockSpec((RET_STEP, 2048), lambda b, s: (row(b, s), COL_G // 2048)),
            pl.BlockSpec((RET_STEP, LANES), lambda b, s: (s, 0)),
            pl.BlockSpec((RET_STEP, LANES), lambda b, s: (s, 0)),
            pl.BlockSpec(dmat.shape, const3),
            pl.BlockSpec(dq.shape, const3),
            pl.BlockSpec(dk.shape, const3),
        ],
        out_specs=pl.BlockSpec((RET_STEP, RET_V_WIDTH), lambda b, s: (row(b, s), 0)),
        scratch_shapes=[pltpu.VMEM((RET_HEADS, RET_QK_DIM, RET_V_DIM), F32)],
        compiler_params=_params("parallel", "arbitrary"),
        name="retention",
    )(proj, proj, proj, proj, cos, sin, dmat, dq, dk)


CONV_STEP = 256
CONV_HALO = 32
CONV_ROWS = 32


def _conv_kernel(val_ref, gate_ref, w_ref, b_ref, lg_ref, lb_ref, o_ref, u_ref):
    @pl.when(pl.program_id(1) == 0)
    def _():
        u_ref[0:CONV_HALO, :] = jnp.zeros((CONV_HALO, CONV_WIDTH), F32)

    u_ref[CONV_HALO:, :] = val_ref[...].astype(F32) * _sigmoid(gate_ref[...].astype(F32))
    shift = CONV_HALO - (CONV_KERNEL - 1)
    for r in range(0, CONV_STEP, CONV_ROWS):
        acc = jnp.zeros((CONV_ROWS, CONV_WIDTH), F32) + b_ref[...]
        for k in range(CONV_KERNEL):
            acc = acc + u_ref[r + shift + k:r + shift + k + CONV_ROWS, :] * w_ref[k:k + 1, :]
        mu = jnp.mean(acc, axis=-1, keepdims=True)
        xc = acc - mu
        y = xc * lax.rsqrt(jnp.mean(xc * xc, axis=-1, keepdims=True) + EPS)
        y = y * lg_ref[...] + lb_ref[...]
        o_ref[r:r + CONV_ROWS, :] = _silu(y).astype(o_ref.dtype)
    u_ref[0:CONV_HALO, :] = u_ref[CONV_STEP:CONV_STEP + CONV_HALO, :]


def _conv_branch(proj, w, b, lg, lb, batch, seq):
    t = proj.shape[0]
    ns = seq // CONV_STEP
    row = lambda bi, s: bi * ns + s
    vec = pl.BlockSpec((1, CONV_WIDTH), lambda bi, s: (0, 0))
    return pl.pallas_call(
        _conv_kernel,
        out_shape=jax.ShapeDtypeStruct((t, CONV_WIDTH), BF16),
        grid=(batch, ns),
        in_specs=[
            pl.BlockSpec((CONV_STEP, CONV_WIDTH), lambda bi, s: (row(bi, s), COL_GLU_VAL // 1024)),
            pl.BlockSpec((CONV_STEP, CONV_WIDTH), lambda bi, s: (row(bi, s), COL_GLU_GATE // 1024)),
            pl.BlockSpec((CONV_KERNEL, CONV_WIDTH), lambda bi, s: (0, 0)),
            vec, vec, vec,
        ],
        out_specs=pl.BlockSpec((CONV_STEP, CONV_WIDTH), lambda bi, s: (row(bi, s), 0)),
        scratch_shapes=[pltpu.VMEM((CONV_STEP + CONV_HALO, CONV_WIDTH), F32)],
        compiler_params=_params("parallel", "arbitrary"),
        name="conv_module",
    )(proj, proj, w, b, lg, lb)


def _merge_kernel(r_ref, u_ref, la_ref, lb_ref, x_ref, wr_ref, wc_ref, bc_ref,
                  bg_ref, wo_ref, o_ref):
    br = jnp.dot(r_ref[...], wr_ref[...], preferred_element_type=F32)
    bc = jnp.dot(u_ref[...], wc_ref[...], preferred_element_type=F32) + bc_ref[...]
    ga = _sigmoid(la_ref[...].astype(F32) + bg_ref[:, :D_MODEL])
    gb = _sigmoid(lb_ref[...].astype(F32) + bg_ref[:, D_MODEL:])
    m = (ga * br + gb * bc).astype(BF16)
    o_ref[...] = x_ref[...] + jnp.dot(m, wo_ref[...], preferred_element_type=F32)


def _merge(r, u, proj, x, w_ret_o, w_conv_o, b_conv_o, b_gate, w_out, bm=512):
    t, d = x.shape
    full = lambda a: pl.BlockSpec(a.shape, lambda i: (0, 0))
    return pl.pallas_call(
        _merge_kernel,
        out_shape=jax.ShapeDtypeStruct((t, d), F32),
        grid=(t // bm,),
        in_specs=[
            pl.BlockSpec((bm, RET_V_WIDTH), lambda i: (i, 0)),
            pl.BlockSpec((bm, CONV_WIDTH), lambda i: (i, 0)),
            pl.BlockSpec((bm, d), lambda i: (i, COL_MERGE // 1024)),
            pl.BlockSpec((bm, d), lambda i: (i, COL_MERGE // 1024 + 1)),
            pl.BlockSpec((bm, d), lambda i: (i, 0)),
            full(w_ret_o), full(w_conv_o), full(b_conv_o), full(b_gate), full(w_out),
        ],
        out_specs=pl.BlockSpec((bm, d), lambda i: (i, 0)),
        compiler_params=_params("parallel"),
        name="merge_out_proj",
    )(r, u, proj, proj, x, w_ret_o, w_conv_o, b_conv_o, b_gate, w_out)


def _ffn_kernel(x_ref, g_ref, w1_ref, w3_ref, w2_ref, o_ref, h_ref, acc_ref):
    j = pl.program_id(1)

    @pl.when(j == 0)
    def _():
        x = x_ref[...]
        h_ref[...] = _rms(x, g_ref[...]).astype(BF16)
        acc_ref[...] = x

    h = h_ref[...]
    a = jnp.dot(h, w1_ref[...], preferred_element_type=F32)
    b = jnp.dot(h, w3_ref[...], preferred_element_type=F32)
    acc_ref[...] += jnp.dot((_silu(a) * b).astype(BF16), w2_ref[...],
                            preferred_element_type=F32)

    @pl.when(j == pl.num_programs(1) - 1)
    def _():
        o_ref[...] = acc_ref[...]


def _ffn(x, g, w1, w3, w2, bm=1024, bf=256):
    t, d = x.shape
    f = w1.shape[1]
    return pl.pallas_call(
        _ffn_kernel,
        out_shape=jax.ShapeDtypeStruct((t, d), F32),
        grid=(t // bm, f // bf),
        in_specs=[pl.BlockSpec((bm, d), lambda i, j: (i, 0)),
                  pl.BlockSpec((1, d), lambda i, j: (0, 0)),
                  pl.BlockSpec((d, bf), lambda i, j: (0, j)),
                  pl.BlockSpec((d, bf), lambda i, j: (0, j)),
                  pl.BlockSpec((bf, d), lambda i, j: (j, 0))],
        out_specs=pl.BlockSpec((bm, d), lambda i, j: (i, 0)),
        scratch_shapes=[pltpu.VMEM((bm, d), BF16), pltpu.VMEM((bm, d), F32)],
        compiler_params=_params("parallel", "arbitrary"),
        name="dense_ffn",
    )(x, g, w1, w3, w2)


def _router_kernel(x_ref, g_ref, wr_ref, o_ref):
    hn = _rms(x_ref[...], g_ref[...])
    logits = jnp.dot(hn, wr_ref[...], preferred_element_type=F32,
                     precision=lax.Precision.HIGHEST)
    lane = lax.broadcasted_iota(jnp.int32, logits.shape, 1)
    neg = jnp.float32(-jnp.inf)
    logits = jnp.where(lane < N_EXPERTS, logits, neg)
    m1 = jnp.max(logits, axis=-1, keepdims=True)
    i1 = jnp.min(jnp.where(logits == m1, lane, LANES), axis=-1, keepdims=True)
    rest = jnp.where(lane == i1, neg, logits)
    m2 = jnp.max(rest, axis=-1, keepdims=True)
    i2 = jnp.min(jnp.where(rest == m2, lane, LANES), axis=-1, keepdims=True)
    e2 = jnp.exp(m2 - m1)
    den = 1.0 + e2
    o_ref[...] = jnp.where(lane == i1, 1.0 / den, 0.0) + jnp.where(lane == i2, e2 / den, 0.0)


def _router(x, g, wr, bm=1024):
    t, d = x.shape
    return pl.pallas_call(
        _router_kernel,
        out_shape=jax.ShapeDtypeStruct((t, LANES), F32),
        grid=(t // bm,),
        in_specs=[pl.BlockSpec((bm, d), lambda i: (i, 0)),
                  pl.BlockSpec((1, d), lambda i: (0, 0)),
                  pl.BlockSpec((d, LANES), lambda i: (0, 0))],
        out_specs=pl.BlockSpec((bm, LANES), lambda i: (i, 0)),
        compiler_params=_params("parallel"),
        name="moe_router",
    )(x, g, wr)


def _moe_kernel(x_ref, g_ref, gate_ref, w1_ref, w3_ref, w2_ref, fg_ref, o_ref,
                h_ref, acc_ref):
    e = pl.program_id(1)

    @pl.when(e == 0)
    def _():
        x = x_ref[...]
        h_ref[...] = _rms(x, g_ref[...]).astype(BF16)
        acc_ref[...] = x

    h = h_ref[...]
    a = jnp.dot(h, w1_ref[0], preferred_element_type=F32)
    b = jnp.dot(h, w3_ref[0], preferred_element_type=F32)
    gates = gate_ref[...]
    lane = lax.broadcasted_iota(jnp.int32, gates.shape, 1)
    ge = jnp.sum(jnp.where(lane == e, gates, 0.0), axis=-1, keepdims=True)
    acc_ref[...] += jnp.dot((_silu(a) * b * ge).astype(BF16), w2_ref[0],
                            preferred_element_type=F32)

    @pl.when(e == pl.num_programs(1) - 1)
    def _():
        o_ref[...] = _rms(acc_ref[...], fg_ref[...])


def _moe(x, g, gates, w1, w3, w2, final_g, bm=512):
    t, d = x.shape
    ne, _, f = w1.shape
    return pl.pallas_call(
        _moe_kernel,
        out_shape=jax.ShapeDtypeStruct((t, d), F32),
        grid=(t // bm, ne),
        in_specs=[pl.BlockSpec((bm, d), lambda i, e: (i, 0)),
                  pl.BlockSpec((1, d), lambda i, e: (0, 0)),
                  pl.BlockSpec((bm, LANES), lambda i, e: (i, 0)),
                  pl.BlockSpec((1, d, f), lambda i, e: (e, 0, 0)),
                  pl.BlockSpec((1, d, f), lambda i, e: (e, 0, 0)),
                  pl.BlockSpec((1, f, d), lambda i, e: (e, 0, 0)),
                  pl.BlockSpec((1, d), lambda i, e: (0, 0))],
        out_specs=pl.BlockSpec((bm, d), lambda i, e: (i, 0)),
        scratch_shapes=[pltpu.VMEM((bm, d), BF16), pltpu.VMEM((bm, d), F32)],
        compiler_params=_params("parallel", "arbitrary"),
        name="moe_experts",
    )(x, g, gates, w1, w3, w2, final_g)


def _rope_tables(seq):
    half = RET_QK_DIM // 2
    pos = jnp.arange(seq, dtype=F32)
    inv_freq = ROPE_BASE ** (-jnp.arange(half, dtype=F32) / half)
    ang = pos[:, None] * inv_freq[None, :]
    return jnp.cos(ang), jnp.sin(ang)


def kernel(x, mix_norm, w_in, b_gate, conv_w, conv_b, conv_ln_g, conv_ln_b, w_ret_o,
           w_conv_o, b_conv_o, w_out, ffn_norm, dense_w1, dense_w3, dense_w2,
           moe_router, moe_w1, moe_w3, moe_w2, final_norm):
    batch, seq, d = x.shape
    depth = w_in.shape[0]
    assert depth == 2 and d == D_MODEL and seq % RET_STEP == 0 and seq % CONV_STEP == 0
    cos, sin = _rope_tables(seq)
    xt = x.reshape(batch * seq, d)
    row = lambda a: a.reshape(1, -1)
    for i in range(depth):
        proj = _proj(xt, row(mix_norm[i]), w_in[i].astype(BF16))
        r = _retention(proj, cos, sin, batch, seq)
        u = _conv_branch(proj, conv_w[i], row(conv_b[i]), row(conv_ln_g[i]),
                         row(conv_ln_b[i]), batch, seq)
        xt = _merge(r, u, proj, xt, w_ret_o[i].astype(BF16), w_conv_o[i].astype(BF16),
                    row(b_conv_o[i]), row(b_gate[i]), w_out[i].astype(BF16))
        j = i // 2
        if i % 2 == 0:
            xt = _ffn(xt, row(ffn_norm[i]), dense_w1[j].astype(BF16),
                      dense_w3[j].astype(BF16), dense_w2[j].astype(BF16))
        else:
            wr = jnp.pad(moe_router[j], ((0, 0), (0, LANES - N_EXPERTS)))
            gates = _router(xt, row(ffn_norm[i]), wr)
            xt = _moe(xt, row(ffn_norm[i]), gates, moe_w1[j].astype(BF16),
                      moe_w3[j].astype(BF16), moe_w2[j].astype(BF16), row(final_norm))
    return xt.reshape(batch, seq, d)
```

```python
import functools

import numpy as np
import jax
import jax.numpy as jnp
from jax import lax
from jax.experimental import pallas as pl
from jax.experimental.pallas import tpu as pltpu

F32 = jnp.float32
BF16 = jnp.bfloat16

D_MODEL = 1024
CHUNK = 64
RET_HEADS = 4
RET_QK_DIM = 256
RET_V_DIM = 512
RET_V_WIDTH = 2048
CONV_WIDTH = 1024
CONV_KERNEL = 31
IN_COLS = 10240
N_EXPERTS = 8
ROPE_BASE = 10000.0
EPS = 1e-6

LANES = 128
VMEM_LIMIT = 56 * 1024 * 1024

COL_Q, COL_K, COL_V, COL_G = 0, 1024, 2048, 4096
COL_GLU_VAL, COL_GLU_GATE, COL_MERGE = 6144, 7168, 8192


def _params(*sem):
    return pltpu.CompilerParams(dimension_semantics=sem, vmem_limit_bytes=VMEM_LIMIT)


def _rms(x, g):
    return x * lax.rsqrt(jnp.mean(x * x, axis=-1, keepdims=True) + EPS) * g


def _sigmoid(x):
    return 1.0 / (1.0 + jnp.exp(-x))


def _silu(x):
    return x * _sigmoid(x)


def _proj_kernel(x_ref, g_ref, w_ref, o_ref, h_ref):
    @pl.when(pl.program_id(1) == 0)
    def _():
        h_ref[...] = _rms(x_ref[...], g_ref[...]).astype(BF16)

    o_ref[...] = jnp.dot(h_ref[...], w_ref[...],
                         preferred_element_type=F32).astype(o_ref.dtype)


def _proj(x, g, w, bm=1024, bn=1024):
    t, d = x.shape
    n = w.shape[1]
    return pl.pallas_call(
        _proj_kernel,
        out_shape=jax.ShapeDtypeStruct((t, n), BF16),
        grid=(t // bm, n // bn),
        in_specs=[pl.BlockSpec((bm, d), lambda i, j: (i, 0)),
                  pl.BlockSpec((1, d), lambda i, j: (0, 0)),
                  pl.BlockSpec((d, bn), lambda i, j: (0, j))],
        out_specs=pl.BlockSpec((bm, bn), lambda i, j: (i, j)),
        scratch_shapes=[pltpu.VMEM((bm, d), BF16)],
        compiler_params=_params("parallel", "arbitrary"),
        name="norm_in_proj",
    )(x, g, w)


RET_BLOCK = 256
RET_STEP = 512


def _retention_tables():
    h = np.arange(RET_HEADS, dtype=np.float64)
    log_g = np.log1p(-np.exp2(-5.0 - h))
    idx = np.arange(RET_BLOCK, dtype=np.float64)
    diff = idx[:, None] - idx[None, :]
    ci, cj = (idx // CHUNK)[:, None], (idx // CHUNK)[None, :]
    expo = np.where(ci == cj, np.abs(diff), diff)
    dmat = np.where(ci >= cj, np.exp(log_g[:, None, None] * expo[None]), 0.0)
    dq = np.exp(log_g[:, None] * (idx[None, :] + 1.0))[..., None]
    dk = np.exp(log_g[:, None] * (RET_BLOCK - 1.0 - idx[None, :]))[..., None]
    block_decay = [float(v) for v in np.exp(log_g * RET_BLOCK)]
    return (jnp.asarray(dmat, F32), jnp.asarray(dq, F32), jnp.asarray(dk, F32),
            block_decay)


def _retention_kernel(block_decay, q_ref, k_ref, v_ref, g_ref, cos_ref, sin_ref,
                      dmat_ref, dq_ref, dk_ref, o_ref, state_ref):
    @pl.when(pl.program_id(1) == 0)
    def _():
        state_ref[...] = jnp.zeros_like(state_ref)

    half = RET_QK_DIM // 2
    for c in range(RET_STEP // RET_BLOCK):
        rows = slice(c * RET_BLOCK, (c + 1) * RET_BLOCK)
        cos = cos_ref[rows, :]
        sin = sin_ref[rows, :]

        def rot(ref, col):
            t1 = ref[rows, col:col + half].astype(F32)
            t2 = ref[rows, col + half:col + 2 * half].astype(F32)
            return jnp.concatenate([t1 * cos - t2 * sin, t1 * sin + t2 * cos], axis=-1)

        for h in range(RET_HEADS):
            q = rot(q_ref, h * RET_QK_DIM).astype(BF16)
            kf = rot(k_ref, h * RET_QK_DIM) * (RET_QK_DIM ** -0.5)
            v = v_ref[rows, h * RET_V_DIM:(h + 1) * RET_V_DIM]
            s = lax.dot_general(q, kf.astype(BF16), (((1,), (1,)), ((), ())),
                                preferred_element_type=F32)
            s = (s * dmat_ref[h]).astype(BF16)
            out = jnp.dot(s, v, preferred_element_type=F32)
            state = state_ref[h]
            out = out + jnp.dot(q, state.astype(BF16),
                                preferred_element_type=F32) * dq_ref[h]
            kd = (kf * dk_ref[h]).astype(BF16)
            kv = lax.dot_general(kd, v, (((0,), (0,)), ((), ())),
                                 preferred_element_type=F32)
            state_ref[h] = state * block_decay[h] + kv
            out = out * lax.rsqrt(jnp.mean(out * out, axis=-1, keepdims=True) + EPS)
            gate = g_ref[rows, h * RET_V_DIM:(h + 1) * RET_V_DIM].astype(F32)
            o_ref[rows, h * RET_V_DIM:(h + 1) * RET_V_DIM] = (
                _silu(gate) * out).astype(o_ref.dtype)


def _retention(proj, cos, sin, batch, seq):
    t = proj.shape[0]
    ns = seq // RET_STEP
    dmat, dq, dk, block_decay = _retention_tables()
    row = lambda b, s: b * ns + s
    const3 = lambda b, s: (0, 0, 0)
    return pl.pallas_call(
        functools.partial(_retention_kernel, block_decay),
        out_shape=jax.ShapeDtypeStruct((t, RET_V_WIDTH), BF16),
        grid=(batch, ns),
        in_specs=[
            pl.BlockSpec((RET_STEP, 1024), lambda b, s: (row(b, s), COL_Q // 1024)),
            pl.BlockSpec((RET_STEP, 1024), lambda b, s: (row(b, s), COL_K // 1024)),
            pl.BlockSpec((RET_STEP, 2048), lambda b, s: (row(b, s), COL_V // 2048)),
            pl.BlockSpec((RET_STEP, 2048), lambda b, s: (row(b, s), COL_G // 2048)),
            pl.BlockSpec((RET_STEP, LANES), lambda b, s: (s, 0)),
            pl.BlockSpec((RET_STEP, LANES), lambda b, s: (s, 0)),
            pl.BlockSpec(dmat.shape, const3),
            pl.BlockSpec(dq.shape, const3),
            pl.BlockSpec(dk.shape, const3),
        ],
        out_specs=pl.BlockSpec((RET_STEP, RET_V_WIDTH), lambda b, s: (row(b, s), 0)),
        scratch_shapes=[pltpu.VMEM((RET_HEADS, RET_QK_DIM, RET_V_DIM), F32)],
        compiler_params=_params("parallel", "arbitrary"),
        name="retention",
    )(proj, proj, proj, proj, cos, sin, dmat, dq, dk)


CONV_STEP = 512
CONV_HALO = 32
CONV_ROWS = 64
SUBLANES = 8
N_SLABS = CONV_WIDTH // LANES


def _conv_kernel(val_ref, gate_ref, w_ref, b_ref, lg_ref, lb_ref, o_ref, u_ref, y_ref):
    @pl.when(pl.program_id(1) == 0)
    def _():
        u_ref[:, 0:2 * CONV_HALO, :] = jnp.zeros((N_SLABS // 2, 2 * CONV_HALO, LANES), F32)

    for c in range(N_SLABS):
        cs = slice(c * LANES, (c + 1) * LANES)
        u = val_ref[:, cs].astype(F32) * _sigmoid(gate_ref[:, cs].astype(F32))
        u_ref[c // 2, pl.ds(2 * CONV_HALO + c % 2, CONV_STEP, stride=2), :] = u

    nv = CONV_ROWS // SUBLANES

    def strip(i, carry):
        r0 = pl.multiple_of(i * CONV_ROWS, CONV_ROWS)
        for c in range(N_SLABS):
            cs = slice(c * LANES, (c + 1) * LANES)
            acc = [jnp.zeros((SUBLANES, LANES), F32) + b_ref[:, cs] for _ in range(nv)]
            for k in range(CONV_KERNEL):
                w = w_ref[k * N_SLABS + c]
                first = CONV_HALO - (CONV_KERNEL - 1) + k
                for v in range(nv):
                    start = 2 * (r0 + first + v * SUBLANES) + c % 2
                    acc[v] = acc[v] + u_ref[c // 2, pl.ds(start, SUBLANES, stride=2), :] * w
            for v in range(nv):
                y_ref[pl.ds(r0 + v * SUBLANES, SUBLANES), cs] = acc[v]
        y = y_ref[pl.ds(r0, CONV_ROWS), :]
        mu = jnp.mean(y, axis=-1, keepdims=True)
        yc = y - mu
        z = yc * lax.rsqrt(jnp.mean(yc * yc, axis=-1, keepdims=True) + EPS)
        z = z * lg_ref[...] + lb_ref[...]
        o_ref[pl.ds(r0, CONV_ROWS), :] = _silu(z).astype(o_ref.dtype)
        return carry

    lax.fori_loop(0, CONV_STEP // CONV_ROWS, strip, 0)
    u_ref[:, 0:2 * CONV_HALO, :] = u_ref[:, 2 * CONV_STEP:2 * (CONV_STEP + CONV_HALO), :]


def _conv_branch(proj, w, b, lg, lb, batch, seq):
    t = proj.shape[0]
    ns = seq // CONV_STEP
    row = lambda bi, s: bi * ns + s
    vec = pl.BlockSpec((1, CONV_WIDTH), lambda bi, s: (0, 0))
    wb = jnp.broadcast_to(w.reshape(CONV_KERNEL * N_SLABS, 1, LANES),
                          (CONV_KERNEL * N_SLABS, SUBLANES, LANES))
    return pl.pallas_call(
        _conv_kernel,
        out_shape=jax.ShapeDtypeStruct((t, CONV_WIDTH), BF16),
        grid=(batch, ns),
        in_specs=[
            pl.BlockSpec((CONV_STEP, CONV_WIDTH), lambda bi, s: (row(bi, s), COL_GLU_VAL // 1024)),
            pl.BlockSpec((CONV_STEP, CONV_WIDTH), lambda bi, s: (row(bi, s), COL_GLU_GATE // 1024)),
            pl.BlockSpec(wb.shape, lambda bi, s: (0, 0, 0)),
            vec, vec, vec,
        ],
        out_specs=pl.BlockSpec((CONV_STEP, CONV_WIDTH), lambda bi, s: (row(bi, s), 0)),
        scratch_shapes=[
            pltpu.VMEM((N_SLABS // 2, 2 * (CONV_STEP + CONV_HALO), LANES), F32),
            pltpu.VMEM((CONV_STEP, CONV_WIDTH), F32),
        ],
        compiler_params=_params("parallel", "arbitrary"),
        name="conv_module",
    )(proj, proj, wb, b, lg, lb)


def _merge_kernel(r_ref, u_ref, la_ref, lb_ref, x_ref, wr_ref, wc_ref, bc_ref,
                  bg_ref, wo_ref, o_ref):
    br = jnp.dot(r_ref[...], wr_ref[...], preferred_element_type=F32)
    bc = jnp.dot(u_ref[...], wc_ref[...], preferred_element_type=F32) + bc_ref[...]
    ga = _sigmoid(la_ref[...].astype(F32) + bg_ref[:, :D_MODEL])
    gb = _sigmoid(lb_ref[...].astype(F32) + bg_ref[:, D_MODEL:])
    m = (ga * br + gb * bc).astype(BF16)
    o_ref[...] = x_ref[...] + jnp.dot(m, wo_ref[...], preferred_element_type=F32)


def _merge(r, u, proj, x, w_ret_o, w_conv_o, b_conv_o, b_gate, w_out, bm=512):
    t, d = x.shape
    full = lambda a: pl.BlockSpec(a.shape, lambda i: (0, 0))
    return pl.pallas_call(
        _merge_kernel,
        out_shape=jax.ShapeDtypeStruct((t, d), F32),
        grid=(t // bm,),
        in_specs=[
            pl.BlockSpec((bm, RET_V_WIDTH), lambda i: (i, 0)),
            pl.BlockSpec((bm, CONV_WIDTH), lambda i: (i, 0)),
            pl.BlockSpec((bm, d), lambda i: (i, COL_MERGE // 1024)),
            pl.BlockSpec((bm, d), lambda i: (i, COL_MERGE // 1024 + 1)),
            pl.BlockSpec((bm, d), lambda i: (i, 0)),
            full(w_ret_o), full(w_conv_o), full(b_conv_o), full(b_gate), full(w_out),
        ],
        out_specs=pl.BlockSpec((bm, d), lambda i: (i, 0)),
        compiler_params=_params("parallel"),
        name="merge_out_proj",
    )(r, u, proj, proj, x, w_ret_o, w_conv_o, b_conv_o, b_gate, w_out)


def _ffn_kernel(x_ref, g_ref, w1_ref, w3_ref, w2_ref, o_ref, h_ref, acc_ref):
    j = pl.program_id(1)

    @pl.when(j == 0)
    def _():
        x = x_ref[...]
        h_ref[...] = _rms(x, g_ref[...]).astype(BF16)
        acc_ref[...] = x

    h = h_ref[...]
    a = jnp.dot(h, w1_ref[...], preferred_element_type=F32)
    b = jnp.dot(h, w3_ref[...], preferred_element_type=F32)
    acc_ref[...] += jnp.dot((_silu(a) * b).astype(BF16), w2_ref[...],
                            preferred_element_type=F32)

    @pl.when(j == pl.num_programs(1) - 1)
    def _():
        o_ref[...] = acc_ref[...]


def _ffn(x, g, w1, w3, w2, bm=1024, bf=256):
    t, d = x.shape
    f = w1.shape[1]
    return pl.pallas_call(
        _ffn_kernel,
        out_shape=jax.ShapeDtypeStruct((t, d), F32),
        grid=(t // bm, f // bf),
        in_specs=[pl.BlockSpec((bm, d), lambda i, j: (i, 0)),
                  pl.BlockSpec((1, d), lambda i, j: (0, 0)),
                  pl.BlockSpec((d, bf), lambda i, j: (0, j)),
                  pl.BlockSpec((d, bf), lambda i, j: (0, j)),
                  pl.BlockSpec((bf, d), lambda i, j: (j, 0))],
        out_specs=pl.BlockSpec((bm, d), lambda i, j: (i, 0)),
        scratch_shapes=[pltpu.VMEM((bm, d), BF16), pltpu.VMEM((bm, d), F32)],
        compiler_params=_params("parallel", "arbitrary"),
        name="dense_ffn",
    )(x, g, w1, w3, w2)


def _router_kernel(x_ref, g_ref, wr_ref, o_ref):
    hn = _rms(x_ref[...], g_ref[...])
    logits = jnp.dot(hn, wr_ref[...], preferred_element_type=F32,
                     precision=lax.Precision.HIGHEST)
    lane = lax.broadcasted_iota(jnp.int32, logits.shape, 1)
    neg = jnp.float32(-jnp.inf)
    logits = jnp.where(lane < N_EXPERTS, logits, neg)
    m1 = jnp.max(logits, axis=-1, keepdims=True)
    i1 = jnp.min(jnp.where(logits == m1, lane, LANES), axis=-1, keepdims=True)
    rest = jnp.where(lane == i1, neg, logits)
    m2 = jnp.max(rest, axis=-1, keepdims=True)
    i2 = jnp.min(jnp.where(rest == m2, lane, LANES), axis=-1, keepdims=True)
    e2 = jnp.exp(m2 - m1)
    den = 1.0 + e2
    o_ref[...] = jnp.where(lane == i1, 1.0 / den, 0.0) + jnp.where(lane == i2, e2 / den, 0.0)


def _router(x, g, wr, bm=1024):
    t, d = x.shape
    return pl.pallas_call(
        _router_kernel,
        out_shape=jax.ShapeDtypeStruct((t, LANES), F32),
        grid=(t // bm,),
        in_specs=[pl.BlockSpec((bm, d), lambda i: (i, 0)),
                  pl.BlockSpec((1, d), lambda i: (0, 0)),
                  pl.BlockSpec((d, LANES), lambda i: (0, 0))],
        out_specs=pl.BlockSpec((bm, LANES), lambda i: (i, 0)),
        compiler_params=_params("parallel"),
        name="moe_router",
    )(x, g, wr)


def _moe_kernel(x_ref, g_ref, gate_ref, w1_ref, w3_ref, w2_ref, fg_ref, o_ref,
                h_ref, acc_ref):
    e = pl.program_id(1)

    @pl.when(e == 0)
    def _():
        x = x_ref[...]
        h_ref[...] = _rms(x, g_ref[...]).astype(BF16)
        acc_ref[...] = x

    h = h_ref[...]
    a = jnp.dot(h, w1_ref[0], preferred_element_type=F32)
    b = jnp.dot(h, w3_ref[0], preferred_element_type=F32)
    gates = gate_ref[...]
    lane = lax.broadcasted_iota(jnp.int32, gates.shape, 1)
    ge = jnp.sum(jnp.where(lane == e, gates, 0.0), axis=-1, keepdims=True)
    acc_ref[...] += jnp.dot((_silu(a) * b * ge).astype(BF16), w2_ref[0],
                            preferred_element_type=F32)

    @pl.when(e == pl.num_programs(1) - 1)
    def _():
        o_ref[...] = _rms(acc_ref[...], fg_ref[...])


def _moe(x, g, gates, w1, w3, w2, final_g, bm=512):
    t, d = x.shape
    ne, _, f = w1.shape
    return pl.pallas_call(
        _moe_kernel,
        out_shape=jax.ShapeDtypeStruct((t, d), F32),
        grid=(t // bm, ne),
        in_specs=[pl.BlockSpec((bm, d), lambda i, e: (i, 0)),
                  pl.BlockSpec((1, d), lambda i, e: (0, 0)),
                  pl.BlockSpec((bm, LANES), lambda i, e: (i, 0)),
                  pl.BlockSpec((1, d, f), lambda i, e: (e, 0, 0)),
                  pl.BlockSpec((1, d, f), lambda i, e: (e, 0, 0)),
                  pl.BlockSpec((1, f, d), lambda i, e: (e, 0, 0)),
                  pl.BlockSpec((1, d), lambda i, e: (0, 0))],
        out_specs=pl.BlockSpec((bm, d), lambda i, e: (i, 0)),
        scratch_shapes=[pltpu.VMEM((bm, d), BF16), pltpu.VMEM((bm, d), F32)],
        compiler_params=_params("parallel", "arbitrary"),
        name="moe_experts",
    )(x, g, gates, w1, w3, w2, final_g)


def _rope_tables(seq):
    half = RET_QK_DIM // 2
    pos = jnp.arange(seq, dtype=F32)
    inv_freq = ROPE_BASE ** (-jnp.arange(half, dtype=F32) / half)
    ang = pos[:, None] * inv_freq[None, :]
    return jnp.cos(ang), jnp.sin(ang)


def kernel(x, mix_norm, w_in, b_gate, conv_w, conv_b, conv_ln_g, conv_ln_b, w_ret_o,
           w_conv_o, b_conv_o, w_out, ffn_norm, dense_w1, dense_w3, dense_w2,
           moe_router, moe_w1, moe_w3, moe_w2, final_norm):
    batch, seq, d = x.shape
    depth = w_in.shape[0]
    assert depth == 2 and d == D_MODEL and seq % RET_STEP == 0 and seq % CONV_STEP == 0
    cos, sin = _rope_tables(seq)
    xt = x.reshape(batch * seq, d)
    row = lambda a: a.reshape(1, -1)
    for i in range(depth):
        proj = _proj(xt, row(mix_norm[i]), w_in[i].astype(BF16))
        r = _retention(proj, cos, sin, batch, seq)
        u = _conv_branch(proj, conv_w[i], row(conv_b[i]), row(conv_ln_g[i]),
                         row(conv_ln_b[i]), batch, seq)
        xt = _merge(r, u, proj, xt, w_ret_o[i].astype(BF16), w_conv_o[i].astype(BF16),
                    row(b_conv_o[i]), row(b_gate[i]), w_out[i].astype(BF16))
        j = i // 2
        if i % 2 == 0:
            xt = _ffn(xt, row(ffn_norm[i]), dense_w1[j].astype(BF16),
                      dense_w3[j].astype(BF16), dense_w2[j].astype(BF16))
        else:
            wr = jnp.pad(moe_router[j], ((0, 0), (0, LANES - N_EXPERTS)))
            gates = _router(xt, row(ffn_norm[i]), wr)
            xt = _moe(xt, row(ffn_norm[i]), gates, moe_w1[j].astype(BF16),
                      moe_w3[j].astype(BF16), moe_w2[j].astype(BF16), row(final_norm))
    return xt.reshape(batch, seq, d)
```

```python
import functools

import numpy as np
import jax
import jax.numpy as jnp
from jax import lax
from jax.experimental import pallas as pl
from jax.experimental.pallas import tpu as pltpu

F32 = jnp.float32
BF16 = jnp.bfloat16

D_MODEL = 1024
CHUNK = 64
RET_HEADS = 4
RET_QK_DIM = 256
RET_V_DIM = 512
RET_V_WIDTH = 2048
CONV_WIDTH = 1024
CONV_KERNEL = 31
N_EXPERTS = 8
ROPE_BASE = 10000.0
EPS = 1e-6

LANES = 128
SUBLANES = 8
VMEM_LIMIT = 56 * 1024 * 1024

PROJ_BN = 2048
PROJ_OUT_COLS = 8192
OUT_COL_K, OUT_COL_V, OUT_COL_G, OUT_COL_MERGE = 1024, 2048, 4096, 6144


def _params(*sem):
    return pltpu.CompilerParams(dimension_semantics=sem, vmem_limit_bytes=VMEM_LIMIT)


def _rms(x, g):
    return x * lax.rsqrt(jnp.mean(x * x, axis=-1, keepdims=True) + EPS) * g


def _sigmoid(x):
    return 1.0 / (1.0 + jnp.exp(-x))


def _silu(x):
    return x * _sigmoid(x)


PROJ_BM = 1024
CONV_HALO = 32
CONV_ROWS = 32
N_SLABS = CONV_WIDTH // LANES
PROJ_STEPS = 5
CHUNKS_PER_STEP = 4
CHUNK_ROWS = PROJ_BM // CHUNKS_PER_STEP
STRIPS_PER_CHUNK = PROJ_BM // CONV_ROWS // (PROJ_STEPS - 1) // CHUNKS_PER_STEP


def _proj_kernel(tiles_per_seq, x_ref, g_ref, w_ref, cos_ref, sin_ref, cw_ref, cb_ref,
                 lg_ref, lb_ref, o_ref, uo_ref, h_ref, u_ref, y_ref):
    i = pl.program_id(0)
    j = pl.program_id(1)

    def project():
        return jnp.dot(h_ref[...], w_ref[...], preferred_element_type=F32)

    def conv_strip(strip):
        r0 = pl.multiple_of(strip * CONV_ROWS, CONV_ROWS)
        nv = CONV_ROWS // SUBLANES
        for c in range(N_SLABS):
            cs = slice(c * LANES, (c + 1) * LANES)
            acc = [jnp.zeros((SUBLANES, LANES), F32) + cb_ref[:, cs] for _ in range(nv)]
            for k in range(CONV_KERNEL):
                w = cw_ref[k * N_SLABS + c]
                first = CONV_HALO - (CONV_KERNEL - 1) + k
                for v in range(nv):
                    start = 2 * (r0 + first + v * SUBLANES) + c % 2
                    acc[v] = acc[v] + u_ref[c // 2, pl.ds(start, SUBLANES, stride=2), :] * w
            for v in range(nv):
                y_ref[v * SUBLANES:(v + 1) * SUBLANES, cs] = acc[v]
        y = y_ref[...]
        mu = jnp.mean(y, axis=-1, keepdims=True)
        yc = y - mu
        z = yc * lax.rsqrt(jnp.mean(yc * yc, axis=-1, keepdims=True) + EPS)
        z = z * lg_ref[...] + lb_ref[...]
        uo_ref[pl.ds(r0, CONV_ROWS), :] = _silu(z).astype(uo_ref.dtype)

    def project_rows(s):
        rows = slice(s * CHUNK_ROWS, (s + 1) * CHUNK_ROWS)
        return rows, jnp.dot(h_ref[rows, :], w_ref[...], preferred_element_type=F32)

    def conv_strips(s):
        for t in range(STRIPS_PER_CHUNK):
            conv_strip(((j - 1) * CHUNKS_PER_STEP + s) * STRIPS_PER_CHUNK + t)

    @pl.when(j == 0)
    def _():
        h_ref[...] = _rms(x_ref[...], g_ref[...]).astype(BF16)

        @pl.when(i % tiles_per_seq == 0)
        def _():
            u_ref[:, 0:2 * CONV_HALO, :] = jnp.zeros((N_SLABS // 2, 2 * CONV_HALO, LANES), F32)

        @pl.when(i % tiles_per_seq != 0)
        def _():
            u_ref[:, 0:2 * CONV_HALO, :] = u_ref[:, 2 * PROJ_BM:2 * (PROJ_BM + CONV_HALO), :]

        r = project()
        for c in range(N_SLABS):
            val = r[:, c * LANES:(c + 1) * LANES]
            gate = r[:, CONV_WIDTH + c * LANES:CONV_WIDTH + (c + 1) * LANES]
            u_ref[c // 2, pl.ds(2 * CONV_HALO + c % 2, PROJ_BM, stride=2), :] = (
                val * _sigmoid(gate))

    @pl.when(j == 1)
    def _():
        half = RET_QK_DIM // 2
        for s in range(CHUNKS_PER_STEP):
            rows, r = project_rows(s)
            cos = cos_ref[rows, :]
            sin = sin_ref[rows, :]
            for part, scale in ((0, 1.0), (1, RET_QK_DIM ** -0.5)):
                cs, sn = cos * scale, sin * scale
                for hd in range(RET_HEADS):
                    lo = part * D_MODEL + hd * RET_QK_DIM
                    t1 = r[:, lo:lo + half]
                    t2 = r[:, lo + half:lo + 2 * half]
                    o_ref[rows, lo:lo + half] = (t1 * cs - t2 * sn).astype(o_ref.dtype)
                    o_ref[rows, lo + half:lo + 2 * half] = (t1 * sn + t2 * cs).astype(o_ref.dtype)
            conv_strips(s)

    @pl.when(j >= 2)
    def _():
        for s in range(CHUNKS_PER_STEP):
            rows, r = project_rows(s)
            o_ref[rows, :] = r.astype(o_ref.dtype)
            conv_strips(s)


def _proj(x, g, w, cos, sin, conv_w, conv_b, ln_g, ln_b, seq):
    t, d = x.shape
    bm = PROJ_BM
    tiles_per_seq = seq // bm
    vec = pl.BlockSpec((1, CONV_WIDTH), lambda i, j: (0, 0))
    wb = jnp.broadcast_to(conv_w.reshape(CONV_KERNEL * N_SLABS, 1, LANES),
                          (CONV_KERNEL * N_SLABS, SUBLANES, LANES))
    w_col = lambda i, j: (0, jnp.where(j == 0, 3, jnp.where(j <= 3, j - 1, 4)))
    o_col = lambda i, j: (i, jnp.maximum(j - 1, 0))
    pos = lambda i, j: (i % tiles_per_seq, 0)
    return pl.pallas_call(
        functools.partial(_proj_kernel, tiles_per_seq),
        out_shape=(jax.ShapeDtypeStruct((t, PROJ_OUT_COLS), BF16),
                   jax.ShapeDtypeStruct((t, CONV_WIDTH), BF16)),
        grid=(t // bm, PROJ_STEPS),
        in_specs=[pl.BlockSpec((bm, d), lambda i, j: (i, 0)),
                  pl.BlockSpec((1, d), lambda i, j: (0, 0)),
                  pl.BlockSpec((d, PROJ_BN), w_col),
                  pl.BlockSpec((bm, LANES), pos),
                  pl.BlockSpec((bm, LANES), pos),
                  pl.BlockSpec(wb.shape, lambda i, j: (0, 0, 0)),
                  vec, vec, vec],
        out_specs=(pl.BlockSpec((bm, PROJ_BN), o_col),
                   pl.BlockSpec((bm, CONV_WIDTH), lambda i, j: (i, 0))),
        scratch_shapes=[
            pltpu.VMEM((bm, d), BF16),
            pltpu.VMEM((N_SLABS // 2, 2 * (bm + CONV_HALO), LANES), F32),
            pltpu.VMEM((CONV_ROWS, CONV_WIDTH), F32),
        ],
        compiler_params=_params("arbitrary", "arbitrary"),
        name="norm_in_proj_conv",
    )(x, g, w, cos, sin, wb, conv_b, ln_g, ln_b)


RET_BLOCK = 256
RET_STEP = 512


def _retention_tables():
    h = np.arange(RET_HEADS, dtype=np.float64)
    log_g = np.log1p(-np.exp2(-5.0 - h))
    idx = np.arange(RET_BLOCK, dtype=np.float64)
    diff = idx[:, None] - idx[None, :]
    ci, cj = (idx // CHUNK)[:, None], (idx // CHUNK)[None, :]
    expo = np.where(ci == cj, np.abs(diff), diff)
    dmat = np.where(ci >= cj, np.exp(log_g[:, None, None] * expo[None]), 0.0)
    dq = np.exp(log_g[:, None] * (idx[None, :] + 1.0))[..., None]
    dk = np.exp(log_g[:, None] * (RET_BLOCK - 1.0 - idx[None, :]))[..., None]
    block_decay = [float(v) for v in np.exp(log_g * RET_BLOCK)]
    return (jnp.asarray(dmat, F32), jnp.asarray(dq, F32), jnp.asarray(dk, F32),
            block_decay)


def _retention_kernel(block_decay, q_ref, k_ref, v_ref, g_ref, dmat_ref, dq_ref,
                      dk_ref, o_ref, state_ref):
    @pl.when(pl.program_id(1) == 0)
    def _():
        state_ref[...] = jnp.zeros_like(state_ref)

    for c in range(RET_STEP // RET_BLOCK):
        rows = slice(c * RET_BLOCK, (c + 1) * RET_BLOCK)
        for h in range(RET_HEADS):
            qk = slice(h * RET_QK_DIM, (h + 1) * RET_QK_DIM)
            vv = slice(h * RET_V_DIM, (h + 1) * RET_V_DIM)
            q = q_ref[rows, qk]
            k = k_ref[rows, qk]
            v = v_ref[rows, vv]
            s = lax.dot_general(q, k, (((1,), (1,)), ((), ())),
                                preferred_element_type=F32)
            s = (s * dmat_ref[h]).astype(BF16)
            out = jnp.dot(s, v, preferred_element_type=F32)
            state = state_ref[h]
            out = out + jnp.dot(q, state.astype(BF16),
                                preferred_element_type=F32) * dq_ref[h]
            kd = (k.astype(F32) * dk_ref[h]).astype(BF16)
            kv = lax.dot_general(kd, v, (((0,), (0,)), ((), ())),
                                 preferred_element_type=F32)
            state_ref[h] = state * block_decay[h] + kv
            out = out * lax.rsqrt(jnp.mean(out * out, axis=-1, keepdims=True) + EPS)
            o_ref[rows, vv] = (_silu(g_ref[rows, vv].astype(F32)) * out).astype(o_ref.dtype)


def _retention(proj, batch, seq):
    t = proj.shape[0]
    ns = seq // RET_STEP
    dmat, dq, dk, block_decay = _retention_tables()
    row = lambda b, s: b * ns + s
    const3 = lambda b, s: (0, 0, 0)
    return pl.pallas_call(
        functools.partial(_retention_kernel, block_decay),
        out_shape=jax.ShapeDtypeStruct((t, RET_V_WIDTH), BF16),
        grid=(batch, ns),
        in_specs=[
            pl.BlockSpec((RET_STEP, 1024), lambda b, s: (row(b, s), 0)),
            pl.BlockSpec((RET_STEP, 1024), lambda b, s: (row(b, s), OUT_COL_K // 1024)),
            pl.BlockSpec((RET_STEP, 2048), lambda b, s: (row(b, s), OUT_COL_V // 2048)),
            pl.BlockSpec((RET_STEP, 2048), lambda b, s: (row(b, s), OUT_COL_G // 2048)),
            pl.BlockSpec(dmat.shape, const3),
            pl.BlockSpec(dq.shape, const3),
            pl.BlockSpec(dk.shape, const3),
        ],
        out_specs=pl.BlockSpec((RET_STEP, RET_V_WIDTH), lambda b, s: (row(b, s), 0)),
        scratch_shapes=[pltpu.VMEM((RET_HEADS, RET_QK_DIM, RET_V_DIM), F32)],
        compiler_params=_params("parallel", "arbitrary"),
        name="retention",
    )(proj, proj, proj, proj, dmat, dq, dk)


def _merge_kernel(r_ref, u_ref, la_ref, lb_ref, x_ref, wr_ref, wc_ref, bc_ref,
                  bg_ref, wo_ref, o_ref):
    br = jnp.dot(r_ref[...], wr_ref[...], preferred_element_type=F32)
    bc = jnp.dot(u_ref[...], wc_ref[...], preferred_element_type=F32) + bc_ref[...]
    ga = _sigmoid(la_ref[...].astype(F32) + bg_ref[:, :D_MODEL])
    gb = _sigmoid(lb_ref[...].astype(F32) + bg_ref[:, D_MODEL:])
    m = (ga * br + gb * bc).astype(BF16)
    o_ref[...] = x_ref[...] + jnp.dot(m, wo_ref[...], preferred_element_type=F32)


def _merge(r, u, proj, x, w_ret_o, w_conv_o, b_conv_o, b_gate, w_out, bm=512):
    t, d = x.shape
    full = lambda a: pl.BlockSpec(a.shape, lambda i: (0, 0))
    return pl.pallas_call(
        _merge_kernel,
        out_shape=jax.ShapeDtypeStruct((t, d), F32),
        grid=(t // bm,),
        in_specs=[
            pl.BlockSpec((bm, RET_V_WIDTH), lambda i: (i, 0)),
            pl.BlockSpec((bm, CONV_WIDTH), lambda i: (i, 0)),
            pl.BlockSpec((bm, d), lambda i: (i, OUT_COL_MERGE // 1024)),
            pl.BlockSpec((bm, d), lambda i: (i, OUT_COL_MERGE // 1024 + 1)),
            pl.BlockSpec((bm, d), lambda i: (i, 0)),
            full(w_ret_o), full(w_conv_o), full(b_conv_o), full(b_gate), full(w_out),
        ],
        out_specs=pl.BlockSpec((bm, d), lambda i: (i, 0)),
        compiler_params=_params("parallel"),
        name="merge_out_proj",
    )(r, u, proj, proj, x, w_ret_o, w_conv_o, b_conv_o, b_gate, w_out)


def _ffn_kernel(gated, final, *refs):
    x_ref, g_ref = refs[0], refs[1]
    refs = refs[2:]
    if gated:
        gate_ref, refs = refs[0], refs[1:]
    w13_ref, w2_ref = refs[0], refs[1]
    refs = refs[2:]
    if final:
        fg_ref, refs = refs[0], refs[1:]
    o_ref, h_ref, acc_ref = refs
    e = pl.program_id(1)

    @pl.when(e == 0)
    def _():
        x = x_ref[...]
        h_ref[...] = _rms(x, g_ref[...]).astype(BF16)
        acc_ref[...] = x

    r = jnp.dot(h_ref[...], w13_ref[0], preferred_element_type=F32)
    f = r.shape[1] // 2
    hid = _silu(r[:, :f]) * r[:, f:]
    if gated:
        gates = gate_ref[...]
        lane = lax.broadcasted_iota(jnp.int32, gates.shape, 1)
        hid = hid * jnp.sum(jnp.where(lane == e, gates, 0.0), axis=-1, keepdims=True)
    acc_ref[...] += jnp.dot(hid.astype(BF16), w2_ref[0], preferred_element_type=F32)

    @pl.when(e == pl.num_programs(1) - 1)
    def _():
        o_ref[...] = _rms(acc_ref[...], fg_ref[...]) if final else acc_ref[...]


def _ffn(x, g, w13, w2, gates=None, final_g=None, bm=512):
    t, d = x.shape
    nc, _, f2 = w13.shape
    gated, final = gates is not None, final_g is not None
    vec = pl.BlockSpec((1, d), lambda i, e: (0, 0))
    in_specs = [pl.BlockSpec((bm, d), lambda i, e: (i, 0)), vec]
    args = [x, g]
    if gated:
        in_specs.append(pl.BlockSpec((bm, LANES), lambda i, e: (i, 0)))
        args.append(gates)
    in_specs += [pl.BlockSpec((1, d, f2), lambda i, e: (e, 0, 0)),
                 pl.BlockSpec((1, f2 // 2, d), lambda i, e: (e, 0, 0))]
    args += [w13, w2]
    if final:
        in_specs.append(vec)
        args.append(final_g)
    return pl.pallas_call(
        functools.partial(_ffn_kernel, gated, final),
        out_shape=jax.ShapeDtypeStruct((t, d), F32),
        grid=(t // bm, nc),
        in_specs=in_specs,
        out_specs=pl.BlockSpec((bm, d), lambda i, e: (i, 0)),
        scratch_shapes=[pltpu.VMEM((bm, d), BF16), pltpu.VMEM((bm, d), F32)],
        compiler_params=_params("parallel", "arbitrary"),
        name="moe_experts" if gated else "dense_ffn",
    )(*args)


def _router_kernel(x_ref, g_ref, wr_ref, o_ref):
    hn = _rms(x_ref[...], g_ref[...])
    logits = jnp.dot(hn, wr_ref[...], preferred_element_type=F32,
                     precision=lax.Precision.HIGHEST)
    lane = lax.broadcasted_iota(jnp.int32, logits.shape, 1)
    neg = jnp.float32(-jnp.inf)
    logits = jnp.where(lane < N_EXPERTS, logits, neg)
    m1 = jnp.max(logits, axis=-1, keepdims=True)
    i1 = jnp.min(jnp.where(logits == m1, lane, LANES), axis=-1, keepdims=True)
    rest = jnp.where(lane == i1, neg, logits)
    m2 = jnp.max(rest, axis=-1, keepdims=True)
    i2 = jnp.min(jnp.where(rest == m2, lane, LANES), axis=-1, keepdims=True)
    e2 = jnp.exp(m2 - m1)
    den = 1.0 + e2
    o_ref[...] = jnp.where(lane == i1, 1.0 / den, 0.0) + jnp.where(lane == i2, e2 / den, 0.0)


def _router(x, g, wr, bm=1024):
    t, d = x.shape
    return pl.pallas_call(
        _router_kernel,
        out_shape=jax.ShapeDtypeStruct((t, LANES), F32),
        grid=(t // bm,),
        in_specs=[pl.BlockSpec((bm, d), lambda i: (i, 0)),
                  pl.BlockSpec((1, d), lambda i: (0, 0)),
                  pl.BlockSpec((d, LANES), lambda i: (0, 0))],
        out_specs=pl.BlockSpec((bm, LANES), lambda i: (i, 0)),
        compiler_params=_params("parallel"),
        name="moe_router",
    )(x, g, wr)


def _rope_tables(seq):
    half = RET_QK_DIM // 2
    pos = jnp.arange(seq, dtype=F32)
    inv_freq = ROPE_BASE ** (-jnp.arange(half, dtype=F32) / half)
    ang = pos[:, None] * inv_freq[None, :]
    return jnp.cos(ang), jnp.sin(ang)


def _split_hidden(w1, w3, w2, chunks):
    d, f = w1.shape
    fc = f // chunks
    to_chunks = lambda w: w.reshape(d, chunks, fc).transpose(1, 0, 2)
    w13 = jnp.concatenate([to_chunks(w1), to_chunks(w3)], axis=-1).astype(BF16)
    return w13, w2.reshape(chunks, fc, d).astype(BF16)


def kernel(x, mix_norm, w_in, b_gate, conv_w, conv_b, conv_ln_g, conv_ln_b, w_ret_o,
           w_conv_o, b_conv_o, w_out, ffn_norm, dense_w1, dense_w3, dense_w2,
           moe_router, moe_w1, moe_w3, moe_w2, final_norm):
    batch, seq, d = x.shape
    depth = w_in.shape[0]
    assert depth == 2 and d == D_MODEL and seq % PROJ_BM == 0 and seq % RET_STEP == 0
    cos, sin = _rope_tables(seq)
    xt = x.reshape(batch * seq, d)
    row = lambda a: a.reshape(1, -1)
    for i in range(depth):
        proj, u = _proj(xt, row(mix_norm[i]), w_in[i].astype(BF16), cos, sin, conv_w[i],
                        row(conv_b[i]), row(conv_ln_g[i]), row(conv_ln_b[i]), seq)
        r = _retention(proj, batch, seq)
        xt = _merge(r, u, proj, xt, w_ret_o[i].astype(BF16), w_conv_o[i].astype(BF16),
                    row(b_conv_o[i]), row(b_gate[i]), w_out[i].astype(BF16))
        j = i // 2
        if i % 2 == 0:
            w13, w2 = _split_hidden(dense_w1[j], dense_w3[j], dense_w2[j], 2)
            xt = _ffn(xt, row(ffn_norm[i]), w13, w2)
        else:
            wr = jnp.pad(moe_router[j], ((0, 0), (0, LANES - N_EXPERTS)))
            gates = _router(xt, row(ffn_norm[i]), wr)
            w13 = jnp.concatenate([moe_w1[j], moe_w3[j]], axis=-1).astype(BF16)
            xt = _ffn(xt, row(ffn_norm[i]), w13, moe_w2[j].astype(BF16), gates=gates,
                      final_g=row(final_norm))
    return xt.reshape(batch, seq, d)
```

```python
import functools

import numpy as np
import jax
import jax.numpy as jnp
from jax import lax
from jax.experimental import pallas as pl
from jax.experimental.pallas import tpu as pltpu
from jax.experimental.pallas import tpu_sc as plsc

F32 = jnp.float32
BF16 = jnp.bfloat16
U32 = jnp.uint32
I32 = jnp.int32

D_MODEL = 1024
CHUNK = 64
RET_HEADS = 4
RET_QK_DIM = 256
RET_V_DIM = 512
RET_V_WIDTH = 2048
CONV_WIDTH = 1024
CONV_KERNEL = 31
N_EXPERTS = 8
TOP_K = 2
ROPE_BASE = 10000.0
EPS = 1e-6

LANES = 128
SUBLANES = 8
VMEM_LIMIT = 56 * 1024 * 1024

PROJ_BN = 2048
PROJ_OUT_COLS = 8192
OUT_COL_K, OUT_COL_V, OUT_COL_G, OUT_COL_MERGE = 1024, 2048, 4096, 6144


def _params(*sem):
    return pltpu.CompilerParams(dimension_semantics=sem, vmem_limit_bytes=VMEM_LIMIT)


def _rms(x, g):
    return x * lax.rsqrt(jnp.mean(x * x, axis=-1, keepdims=True) + EPS) * g


def _sigmoid(x):
    return 1.0 / (1.0 + jnp.exp(-x))


def _silu(x):
    return x * _sigmoid(x)


HIGH_HALF = 0xFFFF0000


def _pack_bf16_pairs(x):
    bits = pltpu.bitcast(x.astype(BF16).astype(F32), U32)
    h = x.shape[1] // 2
    return (bits[:, :h] >> 16) | (bits[:, h:] & U32(HIGH_HALF))


def _unpack_bf16_pairs(p):
    lo = pltpu.bitcast(p << 16, F32)
    hi = pltpu.bitcast(p & U32(HIGH_HALF), F32)
    return jnp.concatenate([lo, hi], axis=-1)


PROJ_BM = 1024
CONV_HALO = 32
CONV_ROWS = 32
N_SLABS = CONV_WIDTH // LANES
PROJ_STEPS = 5
CHUNKS_PER_STEP = 4
CHUNK_ROWS = PROJ_BM // CHUNKS_PER_STEP
STRIPS_PER_CHUNK = PROJ_BM // CONV_ROWS // (PROJ_STEPS - 1) // CHUNKS_PER_STEP


def _proj_kernel(tiles_per_seq, x_ref, g_ref, w_ref, cos_ref, sin_ref, cw_ref, cb_ref,
                 lg_ref, lb_ref, o_ref, uo_ref, h_ref, u_ref, y_ref):
    i = pl.program_id(0)
    j = pl.program_id(1)

    def project():
        return jnp.dot(h_ref[...], w_ref[...], preferred_element_type=F32)

    def conv_strip(strip):
        r0 = pl.multiple_of(strip * CONV_ROWS, CONV_ROWS)
        nv = CONV_ROWS // SUBLANES
        for c in range(N_SLABS):
            cs = slice(c * LANES, (c + 1) * LANES)
            acc = [jnp.zeros((SUBLANES, LANES), F32) + cb_ref[:, cs] for _ in range(nv)]
            for k in range(CONV_KERNEL):
                w = cw_ref[k * N_SLABS + c]
                first = CONV_HALO - (CONV_KERNEL - 1) + k
                for v in range(nv):
                    start = 2 * (r0 + first + v * SUBLANES) + c % 2
                    acc[v] = acc[v] + u_ref[c // 2, pl.ds(start, SUBLANES, stride=2), :] * w
            for v in range(nv):
                y_ref[v * SUBLANES:(v + 1) * SUBLANES, cs] = acc[v]
        y = y_ref[...]
        mu = jnp.mean(y, axis=-1, keepdims=True)
        yc = y - mu
        z = yc * lax.rsqrt(jnp.mean(yc * yc, axis=-1, keepdims=True) + EPS)
        z = z * lg_ref[...] + lb_ref[...]
        uo_ref[pl.ds(r0, CONV_ROWS), :] = _silu(z).astype(uo_ref.dtype)

    def project_rows(s):
        rows = slice(s * CHUNK_ROWS, (s + 1) * CHUNK_ROWS)
        return rows, jnp.dot(h_ref[rows, :], w_ref[...], preferred_element_type=F32)

    def conv_strips(s):
        for t in range(STRIPS_PER_CHUNK):
            conv_strip(((j - 1) * CHUNKS_PER_STEP + s) * STRIPS_PER_CHUNK + t)

    @pl.when(j == 0)
    def _():
        h_ref[...] = _rms(x_ref[...], g_ref[...]).astype(BF16)

        @pl.when(i % tiles_per_seq == 0)
        def _():
            u_ref[:, 0:2 * CONV_HALO, :] = jnp.zeros((N_SLABS // 2, 2 * CONV_HALO, LANES), F32)

        @pl.when(i % tiles_per_seq != 0)
        def _():
            u_ref[:, 0:2 * CONV_HALO, :] = u_ref[:, 2 * PROJ_BM:2 * (PROJ_BM + CONV_HALO), :]

        r = project()
        for c in range(N_SLABS):
            val = r[:, c * LANES:(c + 1) * LANES]
            gate = r[:, CONV_WIDTH + c * LANES:CONV_WIDTH + (c + 1) * LANES]
            u_ref[c // 2, pl.ds(2 * CONV_HALO + c % 2, PROJ_BM, stride=2), :] = (
                val * _sigmoid(gate))

    @pl.when(j == 1)
    def _():
        half = RET_QK_DIM // 2
        for s in range(CHUNKS_PER_STEP):
            rows, r = project_rows(s)
            cos = cos_ref[rows, :]
            sin = sin_ref[rows, :]
            for part, scale in ((0, 1.0), (1, RET_QK_DIM ** -0.5)):
                cs, sn = cos * scale, sin * scale
                for hd in range(RET_HEADS):
                    lo = part * D_MODEL + hd * RET_QK_DIM
                    t1 = r[:, lo:lo + half]
                    t2 = r[:, lo + half:lo + 2 * half]
                    o_ref[rows, lo:lo + half] = (t1 * cs - t2 * sn).astype(o_ref.dtype)
                    o_ref[rows, lo + half:lo + 2 * half] = (t1 * sn + t2 * cs).astype(o_ref.dtype)
            conv_strips(s)

    @pl.when(j >= 2)
    def _():
        for s in range(CHUNKS_PER_STEP):
            rows, r = project_rows(s)
            o_ref[rows, :] = r.astype(o_ref.dtype)
            conv_strips(s)


def _proj(x, g, w, cos, sin, conv_w, conv_b, ln_g, ln_b, seq):
    t, d = x.shape
    bm = PROJ_BM
    tiles_per_seq = seq // bm
    vec = pl.BlockSpec((1, CONV_WIDTH), lambda i, j: (0, 0))
    wb = jnp.broadcast_to(conv_w.reshape(CONV_KERNEL * N_SLABS, 1, LANES),
                          (CONV_KERNEL * N_SLABS, SUBLANES, LANES))
    w_col = lambda i, j: (0, jnp.where(j == 0, 3, jnp.where(j <= 3, j - 1, 4)))
    o_col = lambda i, j: (i, jnp.maximum(j - 1, 0))
    pos = lambda i, j: (i % tiles_per_seq, 0)
    return pl.pallas_call(
        functools.partial(_proj_kernel, tiles_per_seq),
        out_shape=(jax.ShapeDtypeStruct((t, PROJ_OUT_COLS), BF16),
                   jax.ShapeDtypeStruct((t, CONV_WIDTH), BF16)),
        grid=(t // bm, PROJ_STEPS),
        in_specs=[pl.BlockSpec((bm, d), lambda i, j: (i, 0)),
                  pl.BlockSpec((1, d), lambda i, j: (0, 0)),
                  pl.BlockSpec((d, PROJ_BN), w_col),
                  pl.BlockSpec((bm, LANES), pos),
                  pl.BlockSpec((bm, LANES), pos),
                  pl.BlockSpec(wb.shape, lambda i, j: (0, 0, 0)),
                  vec, vec, vec],
        out_specs=(pl.BlockSpec((bm, PROJ_BN), o_col),
                   pl.BlockSpec((bm, CONV_WIDTH), lambda i, j: (i, 0))),
        scratch_shapes=[
            pltpu.VMEM((bm, d), BF16),
            pltpu.VMEM((N_SLABS // 2, 2 * (bm + CONV_HALO), LANES), F32),
            pltpu.VMEM((CONV_ROWS, CONV_WIDTH), F32),
        ],
        compiler_params=_params("arbitrary", "arbitrary"),
        name="norm_in_proj_conv",
    )(x, g, w, cos, sin, wb, conv_b, ln_g, ln_b)


RET_BLOCK = 256
RET_STEP = 512


def _retention_tables():
    h = np.arange(RET_HEADS, dtype=np.float64)
    log_g = np.log1p(-np.exp2(-5.0 - h))
    idx = np.arange(RET_BLOCK, dtype=np.float64)
    diff = idx[:, None] - idx[None, :]
    ci, cj = (idx // CHUNK)[:, None], (idx // CHUNK)[None, :]
    expo = np.where(ci == cj, np.abs(diff), diff)
    dmat = np.where(ci >= cj, np.exp(log_g[:, None, None] * expo[None]), 0.0)
    dq = np.exp(log_g[:, None] * (idx[None, :] + 1.0))[..., None]
    dk = np.exp(log_g[:, None] * (RET_BLOCK - 1.0 - idx[None, :]))[..., None]
    block_decay = [float(v) for v in np.exp(log_g * RET_BLOCK)]
    return (jnp.asarray(dmat, F32), jnp.asarray(dq, F32), jnp.asarray(dk, F32),
            block_decay)


def _retention_kernel(block_decay, q_ref, k_ref, v_ref, g_ref, dmat_ref, dq_ref,
                      dk_ref, o_ref, state_ref):
    @pl.when(pl.program_id(1) == 0)
    def _():
        state_ref[...] = jnp.zeros_like(state_ref)

    for c in range(RET_STEP // RET_BLOCK):
        rows = slice(c * RET_BLOCK, (c + 1) * RET_BLOCK)
        for h in range(RET_HEADS):
            qk = slice(h * RET_QK_DIM, (h + 1) * RET_QK_DIM)
            vv = slice(h * RET_V_DIM, (h + 1) * RET_V_DIM)
            q = q_ref[rows, qk]
            k = k_ref[rows, qk]
            v = v_ref[rows, vv]
            s = lax.dot_general(q, k, (((1,), (1,)), ((), ())),
                                preferred_element_type=F32)
            s = (s * dmat_ref[h]).astype(BF16)
            out = jnp.dot(s, v, preferred_element_type=F32)
            state = state_ref[h]
            out = out + jnp.dot(q, state.astype(BF16),
                                preferred_element_type=F32) * dq_ref[h]
            kd = (k.astype(F32) * dk_ref[h]).astype(BF16)
            kv = lax.dot_general(kd, v, (((0,), (0,)), ((), ())),
                                 preferred_element_type=F32)
            state_ref[h] = state * block_decay[h] + kv
            out = out * lax.rsqrt(jnp.mean(out * out, axis=-1, keepdims=True) + EPS)
            o_ref[rows, vv] = (_silu(g_ref[rows, vv].astype(F32)) * out).astype(o_ref.dtype)


def _retention(proj, batch, seq):
    t = proj.shape[0]
    ns = seq // RET_STEP
    dmat, dq, dk, block_decay = _retention_tables()
    row = lambda b, s: b * ns + s
    const3 = lambda b, s: (0, 0, 0)
    return pl.pallas_call(
        functools.partial(_retention_kernel, block_decay),
        out_shape=jax.ShapeDtypeStruct((t, RET_V_WIDTH), BF16),
        grid=(batch, ns),
        in_specs=[
            pl.BlockSpec((RET_STEP, 1024), lambda b, s: (row(b, s), 0)),
            pl.BlockSpec((RET_STEP, 1024), lambda b, s: (row(b, s), OUT_COL_K // 1024)),
            pl.BlockSpec((RET_STEP, 2048), lambda b, s: (row(b, s), OUT_COL_V // 2048)),
            pl.BlockSpec((RET_STEP, 2048), lambda b, s: (row(b, s), OUT_COL_G // 2048)),
            pl.BlockSpec(dmat.shape, const3),
            pl.BlockSpec(dq.shape, const3),
            pl.BlockSpec(dk.shape, const3),
        ],
        out_specs=pl.BlockSpec((RET_STEP, RET_V_WIDTH), lambda b, s: (row(b, s), 0)),
        scratch_shapes=[pltpu.VMEM((RET_HEADS, RET_QK_DIM, RET_V_DIM), F32)],
        compiler_params=_params("parallel", "arbitrary"),
        name="retention",
    )(proj, proj, proj, proj, dmat, dq, dk)


def _merge_kernel(r_ref, u_ref, la_ref, lb_ref, x_ref, wr_ref, wc_ref, bc_ref,
                  bg_ref, wo_ref, o_ref):
    br = jnp.dot(r_ref[...], wr_ref[...], preferred_element_type=F32)
    bc = jnp.dot(u_ref[...], wc_ref[...], preferred_element_type=F32) + bc_ref[...]
    ga = _sigmoid(la_ref[...].astype(F32) + bg_ref[:, :D_MODEL])
    gb = _sigmoid(lb_ref[...].astype(F32) + bg_ref[:, D_MODEL:])
    m = (ga * br + gb * bc).astype(BF16)
    o_ref[...] = x_ref[...] + jnp.dot(m, wo_ref[...], preferred_element_type=F32)


def _merge(r, u, proj, x, w_ret_o, w_conv_o, b_conv_o, b_gate, w_out, bm=512):
    t, d = x.shape
    full = lambda a: pl.BlockSpec(a.shape, lambda i: (0, 0))
    return pl.pallas_call(
        _merge_kernel,
        out_shape=jax.ShapeDtypeStruct((t, d), F32),
        grid=(t // bm,),
        in_specs=[
            pl.BlockSpec((bm, RET_V_WIDTH), lambda i: (i, 0)),
            pl.BlockSpec((bm, CONV_WIDTH), lambda i: (i, 0)),
            pl.BlockSpec((bm, d), lambda i: (i, OUT_COL_MERGE // 1024)),
            pl.BlockSpec((bm, d), lambda i: (i, OUT_COL_MERGE // 1024 + 1)),
            pl.BlockSpec((bm, d), lambda i: (i, 0)),
            full(w_ret_o), full(w_conv_o), full(b_conv_o), full(b_gate), full(w_out),
        ],
        out_specs=pl.BlockSpec((bm, d), lambda i: (i, 0)),
        compiler_params=_params("parallel"),
        name="merge_out_proj",
    )(r, u, proj, proj, x, w_ret_o, w_conv_o, b_conv_o, b_gate, w_out)


def _swiglu_chunk(h, w13, w2):
    r = jnp.dot(h, w13, preferred_element_type=F32)
    f = r.shape[1] // 2
    hid = _silu(r[:, :f]) * r[:, f:]
    return jnp.dot(hid.astype(BF16), w2, preferred_element_type=F32)


def _ffn_kernel(x_ref, g_ref, w13_ref, w2_ref, o_ref, h_ref, acc_ref):
    e = pl.program_id(1)

    @pl.when(e == 0)
    def _():
        x = x_ref[...]
        h_ref[...] = _rms(x, g_ref[...]).astype(BF16)
        acc_ref[...] = x

    acc_ref[...] += _swiglu_chunk(h_ref[...], w13_ref[0], w2_ref[0])

    @pl.when(e == pl.num_programs(1) - 1)
    def _():
        o_ref[...] = acc_ref[...]


def _ffn(x, g, w13, w2, bm=512):
    t, d = x.shape
    nc, _, f2 = w13.shape
    return pl.pallas_call(
        _ffn_kernel,
        out_shape=jax.ShapeDtypeStruct((t, d), F32),
        grid=(t // bm, nc),
        in_specs=[pl.BlockSpec((bm, d), lambda i, e: (i, 0)),
                  pl.BlockSpec((1, d), lambda i, e: (0, 0)),
                  pl.BlockSpec((1, d, f2), lambda i, e: (e, 0, 0)),
                  pl.BlockSpec((1, f2 // 2, d), lambda i, e: (e, 0, 0))],
        out_specs=pl.BlockSpec((bm, d), lambda i, e: (i, 0)),
        scratch_shapes=[pltpu.VMEM((bm, d), BF16), pltpu.VMEM((bm, d), F32)],
        compiler_params=_params("parallel", "arbitrary"),
        name="dense_ffn",
    )(x, g, w13, w2)


ROUTE_BM = 1024
EXPERT_TILE = 512
SC_WINDOW = 32
ROUTE_E0, ROUTE_E1, ROUTE_R0, ROUTE_R1 = 0, 1, 2, 3


def _router_kernel(x_ref, g_ref, wr_ref, tril_ref, hp_ref, route_ref, gw_ref, cnt_ref,
                   carry_ref):
    @pl.when(pl.program_id(0) == 0)
    def _():
        carry_ref[...] = jnp.zeros_like(carry_ref)

    hn = _rms(x_ref[...], g_ref[...])
    hp_ref[...] = _pack_bf16_pairs(hn)
    logits = jnp.dot(hn, wr_ref[...], preferred_element_type=F32,
                     precision=lax.Precision.HIGHEST)
    lane = lax.broadcasted_iota(I32, logits.shape, 1)
    neg = jnp.float32(-jnp.inf)
    logits = jnp.where(lane < N_EXPERTS, logits, neg)
    m1 = jnp.max(logits, axis=-1, keepdims=True)
    i1 = jnp.min(jnp.where(logits == m1, lane, LANES), axis=-1, keepdims=True)
    rest = jnp.where(lane == i1, neg, logits)
    m2 = jnp.max(rest, axis=-1, keepdims=True)
    i2 = jnp.min(jnp.where(rest == m2, lane, LANES), axis=-1, keepdims=True)
    e2 = jnp.exp(m2 - m1)
    den = 1.0 + e2
    gw_ref[...] = jnp.where(lane == 0, 1.0 / den, jnp.where(lane == 1, e2 / den, 0.0))

    onehot = jnp.where((lane == i1) | (lane == i2), 1.0, 0.0)
    before = carry_ref[...] + jnp.dot(tril_ref[...], onehot.astype(BF16),
                                      preferred_element_type=F32)
    r0 = jnp.sum(jnp.where(lane == i1, before, 0.0), axis=-1, keepdims=True).astype(I32)
    r1 = jnp.sum(jnp.where(lane == i2, before, 0.0), axis=-1, keepdims=True).astype(I32)
    carry_ref[...] += jnp.sum(onehot, axis=0, keepdims=True)
    cnt_ref[...] = carry_ref[...].astype(I32)
    route_ref[...] = jnp.where(
        lane == ROUTE_E0, i1, jnp.where(
            lane == ROUTE_E1, i2, jnp.where(
                lane == ROUTE_R0, r0, jnp.where(lane == ROUTE_R1, r1, 0))))


def _router(x, g, wr):
    t, d = x.shape
    bm = ROUTE_BM
    tril = jnp.tril(jnp.ones((bm, bm), BF16), -1)
    rec = lambda dt: jax.ShapeDtypeStruct((t, LANES), dt)
    return pl.pallas_call(
        _router_kernel,
        out_shape=(jax.ShapeDtypeStruct((t, d // 2), U32), rec(I32), rec(F32),
                   jax.ShapeDtypeStruct((1, LANES), I32)),
        grid=(t // bm,),
        in_specs=[pl.BlockSpec((bm, d), lambda i: (i, 0)),
                  pl.BlockSpec((1, d), lambda i: (0, 0)),
                  pl.BlockSpec((d, LANES), lambda i: (0, 0)),
                  pl.BlockSpec((bm, bm), lambda i: (0, 0))],
        out_specs=(pl.BlockSpec((bm, d // 2), lambda i: (i, 0)),
                   pl.BlockSpec((bm, LANES), lambda i: (i, 0)),
                   pl.BlockSpec((bm, LANES), lambda i: (i, 0)),
                   pl.BlockSpec((1, LANES), lambda i: (0, 0))),
        scratch_shapes=[pltpu.VMEM((1, LANES), F32)],
        compiler_params=_params("arbitrary"),
        name="moe_router",
    )(x, g, wr, tril)


def _sc_mesh():
    return plsc.VectorSubcoreMesh(core_axis_name="c", subcore_axis_name="s")


def _index_windows(idx):
    return idx.reshape(idx.shape[0] // SC_WINDOW, SC_WINDOW)


def _sc_scatter_rows(rows, dest0, dest1, n_out):
    n, d = rows.shape
    idx_spec = pl.BlockSpec((1, SC_WINDOW), lambda i: (i, 0))

    @pl.kernel(out_type=jax.ShapeDtypeStruct((n_out, d), rows.dtype), mesh=_sc_mesh(),
               name="moe_scatter_rows")
    def scatter(x_hbm, i0_hbm, i1_hbm, o_hbm):
        def body(x_vmem, i0_vmem, i1_vmem):
            pltpu.sync_copy(x_vmem, o_hbm.at[i0_vmem.at[0]])
            pltpu.sync_copy(x_vmem, o_hbm.at[i1_vmem.at[0]])

        pltpu.emit_pipeline(
            body,
            grid=(n // SC_WINDOW,),
            in_specs=[pl.BlockSpec((SC_WINDOW, d), lambda i: (i, 0)), idx_spec, idx_spec],
            out_specs=[],
            core_axis_name=("c", "s"),
            dimension_semantics=(pltpu.PARALLEL,),
        )(x_hbm, i0_hbm, i1_hbm)

    return scatter(rows, _index_windows(dest0), _index_windows(dest1))


def _sc_gather_rows(rows, src0, src1):
    n = src0.shape[0]
    d = rows.shape[1]
    idx_spec = pl.BlockSpec((1, SC_WINDOW), lambda i: (i, 0))
    out = jax.ShapeDtypeStruct((n, d), rows.dtype)

    @pl.kernel(out_type=(out, out), mesh=_sc_mesh(), name="moe_gather_rows")
    def gather(y_hbm, i0_hbm, i1_hbm, a_hbm, b_hbm):
        def body(i0_vmem, i1_vmem, a_vmem, b_vmem):
            pltpu.sync_copy(y_hbm.at[i0_vmem.at[0]], a_vmem)
            pltpu.sync_copy(y_hbm.at[i1_vmem.at[0]], b_vmem)

        row_spec = pl.BlockSpec((SC_WINDOW, d), lambda i: (i, 0))
        pltpu.emit_pipeline(
            body,
            grid=(n // SC_WINDOW,),
            in_specs=[idx_spec, idx_spec],
            out_specs=[row_spec, row_spec],
            core_axis_name=("c", "s"),
            dimension_semantics=(pltpu.PARALLEL,),
        )(i0_hbm, i1_hbm, a_hbm, b_hbm)

    return gather(rows, _index_windows(src0), _index_windows(src1))


def _expert_ffn_kernel(tile_expert_ref, n_used_ref, xs_ref, w13_ref, w2_ref, o_ref):
    used = pl.program_id(0) < n_used_ref[0]

    @pl.when(used)
    def _():
        h = _unpack_bf16_pairs(xs_ref[...]).astype(BF16)
        o_ref[...] = _pack_bf16_pairs(_swiglu_chunk(h, w13_ref[0], w2_ref[0]))

    @pl.when(jnp.logical_not(used))
    def _():
        o_ref[...] = jnp.zeros_like(o_ref)


def _expert_ffn(xs, w13, w2, tile_expert, n_used):
    rows, dh = xs.shape
    _, d, f2 = w13.shape
    grid_spec = pltpu.PrefetchScalarGridSpec(
        num_scalar_prefetch=2,
        grid=(rows // EXPERT_TILE,),
        in_specs=[pl.BlockSpec((EXPERT_TILE, dh), lambda k, te, nu: (k, 0)),
                  pl.BlockSpec((1, d, f2), lambda k, te, nu: (te[k], 0, 0)),
                  pl.BlockSpec((1, f2 // 2, d), lambda k, te, nu: (te[k], 0, 0))],
        out_specs=pl.BlockSpec((EXPERT_TILE, dh), lambda k, te, nu: (k, 0)),
    )
    return pl.pallas_call(
        _expert_ffn_kernel,
        out_shape=jax.ShapeDtypeStruct((rows, dh), U32),
        grid_spec=grid_spec,
        compiler_params=_params("arbitrary"),
        name="moe_expert_ffn",
    )(tile_expert, n_used, xs, w13, w2)


def _combine_kernel(x_ref, a_ref, b_ref, gw_ref, fg_ref, o_ref):
    gw = gw_ref[...]
    y = (x_ref[...] + gw[:, 0:1] * _unpack_bf16_pairs(a_ref[...])
         + gw[:, 1:2] * _unpack_bf16_pairs(b_ref[...]))
    o_ref[...] = _rms(y, fg_ref[...])


def _combine(x, ya, yb, gw, final_g, bm=1024):
    t, d = x.shape
    return pl.pallas_call(
        _combine_kernel,
        out_shape=jax.ShapeDtypeStruct((t, d), F32),
        grid=(t // bm,),
        in_specs=[pl.BlockSpec((bm, d), lambda i: (i, 0)),
                  pl.BlockSpec((bm, d // 2), lambda i: (i, 0)),
                  pl.BlockSpec((bm, d // 2), lambda i: (i, 0)),
                  pl.BlockSpec((bm, LANES), lambda i: (i, 0)),
                  pl.BlockSpec((1, d), lambda i: (0, 0))],
        out_specs=pl.BlockSpec((bm, d), lambda i: (i, 0)),
        compiler_params=_params("parallel"),
        name="moe_combine_norm",
    )(x, ya, yb, gw, final_g)


def _moe(x, g, router_w, w1, w3, w2, final_g):
    t = x.shape[0]
    wr = jnp.pad(router_w, ((0, 0), (0, LANES - N_EXPERTS)))
    hp, route, gw, counts = _router(x, g, wr)

    counts = counts[0, :N_EXPERTS]
    padded = (counts + EXPERT_TILE - 1) // EXPERT_TILE * EXPERT_TILE
    ends = jnp.cumsum(padded)
    starts = ends - padded
    dest0 = jnp.take(starts, route[:, ROUTE_E0]) + route[:, ROUTE_R0]
    dest1 = jnp.take(starts, route[:, ROUTE_E1]) + route[:, ROUTE_R1]
    n_rows = TOP_K * t + N_EXPERTS * EXPERT_TILE
    tile_start = jnp.arange(n_rows // EXPERT_TILE, dtype=I32) * EXPERT_TILE
    tile_expert = jnp.minimum(
        jnp.sum(tile_start[:, None] >= ends[None, :], axis=1), N_EXPERTS - 1).astype(I32)
    n_used = (ends[-1:] // EXPERT_TILE).astype(I32)

    xs = _sc_scatter_rows(hp, dest0, dest1, n_rows)
    w13 = jnp.concatenate([w1, w3], axis=-1).astype(BF16)
    ys = _expert_ffn(xs, w13, w2.astype(BF16), tile_expert, n_used)
    ya, yb = _sc_gather_rows(ys, dest0, dest1)
    return _combine(x, ya, yb, gw, final_g)


def _rope_tables(seq):
    half = RET_QK_DIM // 2
    pos = jnp.arange(seq, dtype=F32)
    inv_freq = ROPE_BASE ** (-jnp.arange(half, dtype=F32) / half)
    ang = pos[:, None] * inv_freq[None, :]
    return jnp.cos(ang), jnp.sin(ang)


def _split_hidden(w1, w3, w2, chunks):
    d, f = w1.shape
    fc = f // chunks
    to_chunks = lambda w: w.reshape(d, chunks, fc).transpose(1, 0, 2)
    w13 = jnp.concatenate([to_chunks(w1), to_chunks(w3)], axis=-1).astype(BF16)
    return w13, w2.reshape(chunks, fc, d).astype(BF16)


def kernel(x, mix_norm, w_in, b_gate, conv_w, conv_b, conv_ln_g, conv_ln_b, w_ret_o,
           w_conv_o, b_conv_o, w_out, ffn_norm, dense_w1, dense_w3, dense_w2,
           moe_router, moe_w1, moe_w3, moe_w2, final_norm):
    batch, seq, d = x.shape
    depth = w_in.shape[0]
    assert depth == 2 and d == D_MODEL and seq % PROJ_BM == 0 and seq % RET_STEP == 0
    cos, sin = _rope_tables(seq)
    xt = x.reshape(batch * seq, d)
    row = lambda a: a.reshape(1, -1)
    for i in range(depth):
        proj, u = _proj(xt, row(mix_norm[i]), w_in[i].astype(BF16), cos, sin, conv_w[i],
                        row(conv_b[i]), row(conv_ln_g[i]), row(conv_ln_b[i]), seq)
        r = _retention(proj, batch, seq)
        xt = _merge(r, u, proj, xt, w_ret_o[i].astype(BF16), w_conv_o[i].astype(BF16),
                    row(b_conv_o[i]), row(b_gate[i]), w_out[i].astype(BF16))
        j = i // 2
        if i % 2 == 0:
            w13, w2 = _split_hidden(dense_w1[j], dense_w3[j], dense_w2[j], 2)
            xt = _ffn(xt, row(ffn_norm[i]), w13, w2)
        else:
            xt = _moe(xt, row(ffn_norm[i]), moe_router[j], moe_w1[j], moe_w3[j], moe_w2[j],
                      row(final_norm))
    return xt.reshape(batch, seq, d)
```
